```python
import jax, jax.numpy as jnp
from jax import lax
import numpy as np

D_MODEL = 1024
BATCH = 8
SEQ = 8192
DEPTH = 1
DEC_BATCH = 1
DEC_SEQ = 16384
PAST_LEN = 128

POOL_WIDTH = 256
POOL_WINDOWS = (2, 4, 8, 16)
N_POOL_GROUPS = 4
POOL_GROUP_DIM = POOL_WIDTH // N_POOL_GROUPS
N_HEADS = 6
QK_NOPE_DIM = 128
QK_ROPE_DIM = 64
QK_HEAD_DIM = QK_NOPE_DIM + QK_ROPE_DIM
V_HEAD_DIM = 128
ATTN_WIDTH = N_HEADS * V_HEAD_DIM
MIX_WIDTH = POOL_WIDTH + ATTN_WIDTH
Q_LORA_RANK = 384
KV_LORA_RANK = 256
IN_WIDTH = POOL_WIDTH + Q_LORA_RANK + KV_LORA_RANK + QK_ROPE_DIM
D_FF = -(-8 * D_MODEL // (3 * 256)) * 256
Q_BLOCK = 128
ROPE_THETA = 10000.0
EPS = 1e-6
N_MOD = 6

kernel_name = 'hybrid_pool_mla_sandwich_adaln_encoder'


def rms_norm(x, g):
    xf = x.astype(jnp.float32)
    y = xf * lax.rsqrt(jnp.mean(xf * xf, axis=-1, keepdims=True) + EPS)
    return (y * g.astype(jnp.float32)).astype(x.dtype)


def rope_tables(S):
    inv_freq = ROPE_THETA ** (-jnp.arange(0, QK_ROPE_DIM, 2, dtype=jnp.float32) / QK_ROPE_DIM)
    ang = jnp.arange(S, dtype=jnp.float32)[:, None] * inv_freq[None, :]
    return jnp.cos(ang), jnp.sin(ang)


def apply_rope(x, cos, sin):
    half = QK_ROPE_DIM // 2
    x1, x2 = x[..., :half], x[..., half:]
    cos = cos.astype(x.dtype)
    sin = sin.astype(x.dtype)
    return jnp.concatenate([x1 * cos - x2 * sin, x2 * cos + x1 * sin], axis=-1)


def multiscale_pool(u, pool_w, pool_scale):
    B, S, _ = u.shape
    ug = u.astype(jnp.float32).reshape(B, S, N_POOL_GROUPS, POOL_GROUP_DIM)
    cs = jnp.concatenate([jnp.zeros((B, 1, N_POOL_GROUPS, POOL_GROUP_DIM), jnp.float32),
                          jnp.cumsum(ug, axis=1)], axis=1)
    t = jnp.arange(S)[:, None]
    w = jnp.array(POOL_WINDOWS, dtype=jnp.int32)[None, :]
    lo = jnp.clip(t - w // 2, 0, S)
    hi = jnp.clip(t + w - w // 2, 0, S)
    gidx = jnp.arange(N_POOL_GROUPS)[None, :]
    win_sum = cs[:, hi, gidx] - cs[:, lo, gidx]
    count = (hi - lo).astype(jnp.float32)[None, :, :, None]
    pooled = win_sum / count - ug
    mixed = jnp.einsum('bsgc,gcd->bsgd', pooled.astype(u.dtype), pool_w)
    return mixed.reshape(B, S, POOL_WIDTH) * pool_scale


def mla_attention(cq, ckv, k_pe, g_q_a, w_uq, g_kv_a, w_ukv):
    B, S, _ = cq.shape
    cos, sin = rope_tables(S)
    q = (rms_norm(cq, g_q_a) @ w_uq).reshape(B, S, N_HEADS, QK_HEAD_DIM)
    kv = (rms_norm(ckv, g_kv_a) @ w_ukv).reshape(B, S, N_HEADS, QK_NOPE_DIM + V_HEAD_DIM)
    scale = QK_HEAD_DIM ** -0.5
    q_nope = q[..., :QK_NOPE_DIM] * scale
    q_pe = apply_rope(q[..., QK_NOPE_DIM:], cos[:, None, :], sin[:, None, :]) * scale
    k_nope = kv[..., :QK_NOPE_DIM]
    v = kv[..., QK_NOPE_DIM:]
    k_rot = apply_rope(k_pe, cos, sin)
    nb = S // Q_BLOCK
    qn_b = q_nope.reshape(B, nb, Q_BLOCK, N_HEADS, QK_NOPE_DIM).transpose(1, 0, 2, 3, 4)
    qr_b = q_pe.reshape(B, nb, Q_BLOCK, N_HEADS, QK_ROPE_DIM).transpose(1, 0, 2, 3, 4)

    def block(args):
        qn, qr = args
        s = (jnp.einsum('bqhd,bkhd->bhqk', qn, k_nope)
             + jnp.einsum('bqhr,bkr->bhqk', qr, k_rot)).astype(jnp.float32)
        p = jax.nn.softmax(s, axis=-1).astype(v.dtype)
        return jnp.einsum('bhqk,bkhd->bqhd', p, v)

    o = lax.map(block, (qn_b, qr_b))
    return o.transpose(1, 0, 2, 3, 4).reshape(B, S, ATTN_WIDTH)


def encoder_layer(x, c, w_ada, b_ada, g_mix_pre, g_mix_post, w_in, pool_w, pool_scale,
                  g_q_a, w_uq, g_kv_a, w_ukv, g_pool_out, g_attn_out, w_out,
                  g_ffn_pre, g_ffn_post, w_gate, w_up, w_down):
    B, S, D = x.shape
    mod = (jax.nn.silu(c) @ w_ada + b_ada).reshape(B, N_MOD, 1, D)
    shift1, scale1, gate1 = mod[:, 0], mod[:, 1], mod[:, 2]
    shift2, scale2, gate2 = mod[:, 3], mod[:, 4], mod[:, 5]

    h = rms_norm(x, g_mix_pre) * (1 + scale1) + shift1
    z = h @ w_in
    u, cq, ckv, k_pe = jnp.split(
        z, [POOL_WIDTH, POOL_WIDTH + Q_LORA_RANK, POOL_WIDTH + Q_LORA_RANK + KV_LORA_RANK], axis=-1)
    pool_out = multiscale_pool(u, pool_w, pool_scale)
    attn_out = mla_attention(cq, ckv, k_pe, g_q_a, w_uq, g_kv_a, w_ukv)
    merged = jnp.concatenate([rms_norm(pool_out, g_pool_out),
                              rms_norm(attn_out, g_attn_out)], axis=-1) @ w_out
    x = x + gate1 * rms_norm(merged, g_mix_post)

    h = rms_norm(x, g_ffn_pre) * (1 + scale2) + shift2
    f = (jax.nn.silu(h @ w_gate) * (h @ w_up)) @ w_down
    return x + gate2 * rms_norm(f, g_ffn_post)


def setup_inputs(seed: int = 0) -> dict:
    key = jax.random.key(seed)
    ks = jax.random.split(key, 32)
    f32 = jnp.float32
    L = DEPTH

    def nrm(k, shape, scale):
        return jax.random.normal(k, shape, f32) * scale

    def gain(k, n):
        return 1.0 + 0.05 * jax.random.normal(k, (L, n), f32)

    return {
        'x_prompt': nrm(ks[0], (BATCH, SEQ, D_MODEL), 1.0),
        'x_sample': nrm(ks[1], (DEC_BATCH, DEC_SEQ, D_MODEL), 1.0),
        'c_prompt': nrm(ks[2], (BATCH, D_MODEL), 1.0),
        'c_sample': nrm(ks[3], (DEC_BATCH, D_MODEL), 1.0),
        'w_ada': nrm(ks[4], (L, D_MODEL, N_MOD * D_MODEL), D_MODEL ** -0.5),
        'b_ada': nrm(ks[5], (L, N_MOD * D_MODEL), 0.1),
        'g_mix_pre': gain(ks[6], D_MODEL),
        'g_mix_post': gain(ks[7], D_MODEL),
        'w_in': nrm(ks[8], (L, D_MODEL, IN_WIDTH), D_MODEL ** -0.5),
        'pool_w': nrm(ks[9], (L, N_POOL_GROUPS, POOL_GROUP_DIM, POOL_GROUP_DIM), POOL_GROUP_DIM ** -0.5),
        'pool_scale': gain(ks[10], POOL_WIDTH),
        'g_q_a': gain(ks[11], Q_LORA_RANK),
        'w_uq': nrm(ks[12], (L, Q_LORA_RANK, N_HEADS * QK_HEAD_DIM), Q_LORA_RANK ** -0.5),
        'g_kv_a': gain(ks[13], KV_LORA_RANK),
        'w_ukv': nrm(ks[14], (L, KV_LORA_RANK, N_HEADS * (QK_NOPE_DIM + V_HEAD_DIM)), KV_LORA_RANK ** -0.5),
        'g_pool_out': gain(ks[15], POOL_WIDTH),
        'g_attn_out': gain(ks[16], ATTN_WIDTH),
        'w_out': nrm(ks[17], (L, MIX_WIDTH, D_MODEL), MIX_WIDTH ** -0.5),
        'g_ffn_pre': gain(ks[18], D_MODEL),
        'g_ffn_post': gain(ks[19], D_MODEL),
        'w_gate': nrm(ks[20], (L, D_MODEL, D_FF), D_MODEL ** -0.5),
        'w_up': nrm(ks[21], (L, D_MODEL, D_FF), D_MODEL ** -0.5),
        'w_down': nrm(ks[22], (L, D_FF, D_MODEL), D_FF ** -0.5),
    }


def reference(x_prompt, x_sample, c_prompt, c_sample, w_ada, b_ada, g_mix_pre, g_mix_post,
              w_in, pool_w, pool_scale, g_q_a, w_uq, g_kv_a, w_ukv, g_pool_out, g_attn_out,
              w_out, g_ffn_pre, g_ffn_post, w_gate, w_up, w_down):
    def run(x, c):
        for l in range(DEPTH):
            x = encoder_layer(x, c, w_ada[l], b_ada[l], g_mix_pre[l], g_mix_post[l], w_in[l],
                              pool_w[l], pool_scale[l], g_q_a[l], w_uq[l], g_kv_a[l], w_ukv[l],
                              g_pool_out[l], g_attn_out[l], w_out[l], g_ffn_pre[l], g_ffn_post[l],
                              w_gate[l], w_up[l], w_down[l])
        return x

    y_prompt = run(x_prompt, c_prompt)
    y_sample = run(x_sample, c_sample)
    return (y_prompt, y_sample)
```

```python
import functools
import math

import jax
import jax.numpy as jnp
from jax import lax
from jax.experimental import pallas as pl
from jax.experimental.pallas import tpu as pltpu

D_MODEL = 1024
POOL_WIDTH = 256
POOL_WINDOWS = (2, 4, 8, 16)
POOL_GROUP_DIM = 64
N_HEADS = 6
QK_NOPE_DIM = 128
QK_ROPE_DIM = 64
QK_HEAD_DIM = QK_NOPE_DIM + QK_ROPE_DIM
V_HEAD_DIM = 128
ATTN_WIDTH = N_HEADS * V_HEAD_DIM
Q_LORA_RANK = 384
KV_LORA_RANK = 256
IN_WIDTH = POOL_WIDTH + Q_LORA_RANK + KV_LORA_RANK + QK_ROPE_DIM
D_FF = 2816
ROPE_THETA = 10000.0
EPS = 1e-6
N_MOD = 6

LANES = 128
HEAD_PAD = 2 * LANES
IN_PAD = 1024
POOL_HALO = 8
FF_CHUNKS = ((0, 1024), (1024, 2048), (2048, 2816))
VMEM_LIMIT = 56 * 1024 * 1024

TOKEN_TILE = 512
Q_TILE = 512
KV_TILE = 512
ROPE_TILE = 2048
ADA_TILE = 1024

_F32 = jnp.float32
_BF16 = jnp.bfloat16


def _rms(x, g):
    ms = jnp.mean(x * x, axis=-1, keepdims=True)
    return x * lax.rsqrt(ms + EPS) * g


def _silu(x):
    return x * (1.0 / (1.0 + jnp.exp(-x)))


def _rotate(r, ct, sa, sb):
    return r * ct + pltpu.roll(r, 96, 1) * sa + pltpu.roll(r, 32, 1) * sb


def _adaln_kernel(c_ref, w_ref, b_ref, o_ref):
    sc = _silu(c_ref[...])
    o_ref[...] = jnp.dot(sc, w_ref[...], precision=lax.Precision.HIGHEST,
                         preferred_element_type=_F32) + b_ref[...]


def _adaln(c, w_ada, b_ada):
    nb = c.shape[0]
    n_out = w_ada.shape[1]
    return pl.pallas_call(
        _adaln_kernel,
        grid=(n_out // ADA_TILE,),
        in_specs=[
            pl.BlockSpec((nb, D_MODEL), lambda j: (0, 0)),
            pl.BlockSpec((D_MODEL, ADA_TILE), lambda j: (0, j)),
            pl.BlockSpec((1, ADA_TILE), lambda j: (0, j)),
        ],
        out_specs=pl.BlockSpec((nb, ADA_TILE), lambda j: (0, j)),
        out_shape=jax.ShapeDtypeStruct((nb, n_out), _F32),
        name="adaln",
    )(c, w_ada, b_ada.reshape(1, n_out))


def _rope_table_kernel(ct_ref, sa_ref, sb_ref):
    rows = ct_ref.shape[0]
    base = pl.program_id(0) * rows
    pos = (lax.broadcasted_iota(jnp.int32, (rows, LANES), 0) + base).astype(_F32)
    lane = lax.broadcasted_iota(jnp.int32, (rows, LANES), 1)
    freq = (2 * (lane & 31)).astype(_F32)
    inv_freq = jnp.power(jnp.float32(ROPE_THETA), -freq / QK_ROPE_DIM)
    ang = pos * inv_freq
    c = jnp.cos(ang)
    s = jnp.sin(ang)
    ct_ref[...] = jnp.where(lane < 64, c, 0.0)
    sa_ref[...] = jnp.where(lane < 32, -s, 0.0)
    sb_ref[...] = jnp.where(lane < 32, 0.0, jnp.where(lane < 64, s, 0.0))


def _rope_tables(seq):
    tile = min(ROPE_TILE, seq)
    spec = pl.BlockSpec((tile, LANES), lambda i: (i, 0))
    shape = jax.ShapeDtypeStruct((seq, LANES), _F32)
    return pl.pallas_call(
        _rope_table_kernel,
        grid=(seq // tile,),
        in_specs=[],
        out_specs=[spec, spec, spec],
        out_shape=[shape, shape, shape],
        name="rope_table",
    )()


def _pre_mixer_kernel(x_ref, mod_ref, g_pre_ref, w_in_ref, g_q_ref, w_uq_ref, g_kv_ref, w_ukv_ref,
                      ct_ref, sa_ref, sb_ref,
                      u_ref, q_ref, kn_ref, kr_ref, vt_ref):
    x = x_ref[0]
    mod = mod_ref[0]
    shift1 = mod[0:1]
    scale1 = mod[1:2]
    h = _rms(x, g_pre_ref[...]) * (1.0 + scale1) + shift1
    z = jnp.dot(h.astype(_BF16), w_in_ref[...], preferred_element_type=_F32)

    u_ref[0] = z[:, :POOL_WIDTH]

    ct = ct_ref[...]
    sa = sa_ref[...]
    sb = sb_ref[...]
    qk_scale = QK_HEAD_DIM ** -0.5

    cq = z[:, POOL_WIDTH:POOL_WIDTH + Q_LORA_RANK]
    q = jnp.dot(_rms(cq, g_q_ref[...]).astype(_BF16), w_uq_ref[...], preferred_element_type=_F32)
    for hd in range(N_HEADS):
        lo = hd * HEAD_PAD
        q_nope = q[:, lo:lo + LANES] * qk_scale
        q_rope = _rotate(q[:, lo + LANES:lo + HEAD_PAD], ct, sa, sb) * qk_scale
        q_ref[0, hd, :, 0:LANES] = q_nope.astype(_BF16)
        q_ref[0, hd, :, LANES:HEAD_PAD] = q_rope.astype(_BF16)

    c0 = POOL_WIDTH + Q_LORA_RANK
    ckv = z[:, c0:c0 + KV_LORA_RANK]
    kv = jnp.dot(_rms(ckv, g_kv_ref[...]).astype(_BF16), w_ukv_ref[...], preferred_element_type=_F32)
    for hd in range(N_HEADS):
        lo = hd * (QK_NOPE_DIM + V_HEAD_DIM)
        kn_ref[0, hd] = kv[:, lo:lo + QK_NOPE_DIM].astype(_BF16)
        vt_ref[0, hd] = kv[:, lo + QK_NOPE_DIM:lo + QK_NOPE_DIM + V_HEAD_DIM].T.astype(_BF16)

    k_pe = z[:, c0 + KV_LORA_RANK:IN_PAD]
    kr_ref[0] = _rotate(k_pe, ct, sa, sb).astype(_BF16)


def _pre_mixer(x, mod, g_pre, w_in_p, g_q, w_uq_p, g_kv, w_ukv_b, tables):
    b, s, _ = x.shape
    tm = min(TOKEN_TILE, s)
    const = lambda shape: pl.BlockSpec(shape, lambda bi, i: (0,) * len(shape))
    tab = pl.BlockSpec((tm, LANES), lambda bi, i: (i, 0))
    return pl.pallas_call(
        _pre_mixer_kernel,
        grid=(b, s // tm),
        in_specs=[
            pl.BlockSpec((1, tm, D_MODEL), lambda bi, i: (bi, i, 0)),
            pl.BlockSpec((1, N_MOD, D_MODEL), lambda bi, i: (bi, 0, 0)),
            const((1, D_MODEL)),
            const((D_MODEL, IN_PAD)),
            const((1, Q_LORA_RANK)),
            const((Q_LORA_RANK, N_HEADS * HEAD_PAD)),
            const((1, KV_LORA_RANK)),
            const((KV_LORA_RANK, N_HEADS * (QK_NOPE_DIM + V_HEAD_DIM))),
            tab, tab, tab,
        ],
        out_specs=[
            pl.BlockSpec((1, tm, POOL_WIDTH), lambda bi, i: (bi, i, 0)),
            pl.BlockSpec((1, N_HEADS, tm, HEAD_PAD), lambda bi, i: (bi, 0, i, 0)),
            pl.BlockSpec((1, N_HEADS, tm, QK_NOPE_DIM), lambda bi, i: (bi, 0, i, 0)),
            pl.BlockSpec((1, tm, LANES), lambda bi, i: (bi, i, 0)),
            pl.BlockSpec((1, N_HEADS, V_HEAD_DIM, tm), lambda bi, i: (bi, 0, 0, i)),
        ],
        out_shape=[
            jax.ShapeDtypeStruct((b, s, POOL_WIDTH), _F32),
            jax.ShapeDtypeStruct((b, N_HEADS, s, HEAD_PAD), _BF16),
            jax.ShapeDtypeStruct((b, N_HEADS, s, QK_NOPE_DIM), _BF16),
            jax.ShapeDtypeStruct((b, s, LANES), _BF16),
            jax.ShapeDtypeStruct((b, N_HEADS, V_HEAD_DIM, s), _BF16),
        ],
        compiler_params=pltpu.CompilerParams(
            dimension_semantics=("parallel", "parallel"), vmem_limit_bytes=VMEM_LIMIT),
        name="pre_mixer",
    )(x, mod, g_pre, w_in_p, g_q, w_uq_p, g_kv, w_ukv_b, *tables)


def _attention_kernel(q_ref, kn_ref, kr_ref, vt_ref, g_ref, o_ref, m_ref, l_ref, acc_ref):
    j = pl.program_id(2)

    @pl.when(j == 0)
    def _():
        m_ref[...] = jnp.full(m_ref.shape, -1e30, _F32)
        l_ref[...] = jnp.zeros(l_ref.shape, _F32)
        acc_ref[...] = jnp.zeros(acc_ref.shape, _F32)

    kr = kr_ref[0]
    for hd in range(N_HEADS):
        k = jnp.concatenate([kn_ref[0, hd], kr], axis=1)
        st = lax.dot_general(k, q_ref[0, hd], (((1,), (1,)), ((), ())),
                             preferred_element_type=_F32)
        m_old = m_ref[hd]
        m_new = jnp.maximum(m_old, jnp.max(st, axis=0, keepdims=True))
        alpha = jnp.exp(m_old - m_new)
        p = jnp.exp(st - m_new)
        l_ref[hd] = alpha * l_ref[hd] + jnp.sum(p, axis=0, keepdims=True)
        pv = jnp.dot(vt_ref[0, hd], p.astype(_BF16), preferred_element_type=_F32)
        acc_ref[hd] = alpha * acc_ref[hd] + pv
        m_ref[hd] = m_new

    @pl.when(j == pl.num_programs(2) - 1)
    def _():
        outs = [acc_ref[hd] * (1.0 / l_ref[hd]) for hd in range(N_HEADS)]
        ssq = outs[0] * outs[0]
        for hd in range(1, N_HEADS):
            ssq = ssq + outs[hd] * outs[hd]
        inv = lax.rsqrt(jnp.sum(ssq, axis=0, keepdims=True) * (1.0 / ATTN_WIDTH) + EPS)
        for hd in range(N_HEADS):
            g = g_ref[:, hd * V_HEAD_DIM:(hd + 1) * V_HEAD_DIM]
            o_ref[0, :, hd * V_HEAD_DIM:(hd + 1) * V_HEAD_DIM] = ((outs[hd] * inv).T * g).astype(_BF16)


def _attention(q, kn, kr, vt, g_attn):
    b, _, s, _ = q.shape
    tq = min(Q_TILE, s)
    tk = min(KV_TILE, s)
    return pl.pallas_call(
        _attention_kernel,
        grid=(b, s // tq, s // tk),
        in_specs=[
            pl.BlockSpec((1, N_HEADS, tq, HEAD_PAD), lambda bi, i, j: (bi, 0, i, 0)),
            pl.BlockSpec((1, N_HEADS, tk, QK_NOPE_DIM), lambda bi, i, j: (bi, 0, j, 0)),
            pl.BlockSpec((1, tk, LANES), lambda bi, i, j: (bi, j, 0)),
            pl.BlockSpec((1, N_HEADS, V_HEAD_DIM, tk), lambda bi, i, j: (bi, 0, 0, j)),
            pl.BlockSpec((1, ATTN_WIDTH), lambda bi, i, j: (0, 0)),
        ],
        out_specs=pl.BlockSpec((1, tq, ATTN_WIDTH), lambda bi, i, j: (bi, i, 0)),
        out_shape=jax.ShapeDtypeStruct((b, s, ATTN_WIDTH), _BF16),
        scratch_shapes=[
            pltpu.VMEM((N_HEADS, 1, tq), _F32),
            pltpu.VMEM((N_HEADS, 1, tq), _F32),
            pltpu.VMEM((N_HEADS, V_HEAD_DIM, tq), _F32),
        ],
        compiler_params=pltpu.CompilerParams(
            dimension_semantics=("parallel", "parallel", "arbitrary"), vmem_limit_bytes=VMEM_LIMIT),
        name="attention",
    )(q, kn, kr, vt, g_attn)


def _post_mixer_kernel(x_ref, u_ref, up_ref, un_ref, a_ref, mod_ref,
                       pw_ref, ps_ref, g_pool_ref, w_out_ref, g_post_ref,
                       g_fpre_ref, g_fpost_ref, wg_ref, wu_ref, wd_ref, y_ref):
    i = pl.program_id(1)
    n_tiles = pl.num_programs(1)
    tm = x_ref.shape[1]
    seq = tm * n_tiles
    mod = mod_ref[0]
    gate1, shift2, scale2, gate2 = mod[2:3], mod[3:4], mod[4:5], mod[5:6]

    u = u_ref[0]
    prev = jnp.where(i > 0, up_ref[0], 0.0)
    nxt = jnp.where(i < n_tiles - 1, un_ref[0], 0.0)
    ext = jnp.concatenate([prev, u, nxt], axis=0)
    rows = tm + 2 * POOL_HALO
    back = lambda a, k: pltpu.roll(a, k, 0)
    fwd = lambda a, k: pltpu.roll(a, rows - k, 0)
    s2 = ext + back(ext, 1)
    s4 = back(s2, 1) + fwd(s2, 1)
    s8 = back(s4, 2) + fwd(s4, 2)
    s16 = back(s8, 4) + fwd(s8, 4)
    lane = lax.broadcasted_iota(jnp.int32, (tm, POOL_WIDTH), 1)
    grp = lane // POOL_GROUP_DIM
    body = slice(POOL_HALO, POOL_HALO + tm)
    win = jnp.where(grp == 0, s2[body], jnp.where(grp == 1, s4[body],
                    jnp.where(grp == 2, s8[body], s16[body])))
    half = jnp.where(grp == 0, 1, jnp.where(grp == 1, 2, jnp.where(grp == 2, 4, 8)))
    t = lax.broadcasted_iota(jnp.int32, (tm, POOL_WIDTH), 0) + i * tm
    count = (jnp.minimum(t + half, seq) - jnp.maximum(t - half, 0)).astype(_F32)
    pooled = win / count - u
    mixed = jnp.dot(pooled.astype(_BF16), pw_ref[...], preferred_element_type=_F32) * ps_ref[...]
    pool_n = _rms(mixed, g_pool_ref[...])

    merged = jnp.dot(pool_n.astype(_BF16), w_out_ref[0:POOL_WIDTH, :], preferred_element_type=_F32)
    merged = merged + jnp.dot(a_ref[0], w_out_ref[POOL_WIDTH:, :], preferred_element_type=_F32)
    x1 = x_ref[0] + gate1 * _rms(merged, g_post_ref[...])

    h = (_rms(x1, g_fpre_ref[...]) * (1.0 + scale2) + shift2).astype(_BF16)
    f = None
    for lo, hi in FF_CHUNKS:
        gt = jnp.dot(h, wg_ref[:, lo:hi], preferred_element_type=_F32)
        up = jnp.dot(h, wu_ref[:, lo:hi], preferred_element_type=_F32)
        act = (_silu(gt) * up).astype(_BF16)
        part = jnp.dot(act, wd_ref[lo:hi, :], preferred_element_type=_F32)
        f = part if f is None else f + part
    y_ref[0] = x1 + gate2 * _rms(f, g_fpost_ref[...])


def _post_mixer(x, u, attn_n, mod, pool_w_bd, pool_scale, g_pool, w_out_b, g_post,
                g_fpre, g_fpost, w_gate_b, w_up_b, w_down_b):
    b, s, _ = x.shape
    tm = min(TOKEN_TILE, s)
    hb = tm // POOL_HALO
    last_hb = s // POOL_HALO - 1
    const = lambda shape: pl.BlockSpec(shape, lambda bi, i: (0,) * len(shape),
                                       pipeline_mode=pl.Buffered(1))
    return pl.pallas_call(
        _post_mixer_kernel,
        grid=(b, s // tm),
        in_specs=[
            pl.BlockSpec((1, tm, D_MODEL), lambda bi, i: (bi, i, 0)),
            pl.BlockSpec((1, tm, POOL_WIDTH), lambda bi, i: (bi, i, 0)),
            pl.BlockSpec((1, POOL_HALO, POOL_WIDTH), lambda bi, i: (bi, jnp.maximum(i * hb - 1, 0), 0)),
            pl.BlockSpec((1, POOL_HALO, POOL_WIDTH),
                         lambda bi, i: (bi, jnp.minimum((i + 1) * hb, last_hb), 0)),
            pl.BlockSpec((1, tm, ATTN_WIDTH), lambda bi, i: (bi, i, 0)),
            pl.BlockSpec((1, N_MOD, D_MODEL), lambda bi, i: (bi, 0, 0)),
            const((POOL_WIDTH, POOL_WIDTH)),
            const((1, POOL_WIDTH)),
            const((1, POOL_WIDTH)),
            const((D_MODEL, D_MODEL)),
            const((1, D_MODEL)),
            const((1, D_MODEL)),
            const((1, D_MODEL)),
            const((D_MODEL, D_FF)),
            const((D_MODEL, D_FF)),
            const((D_FF, D_MODEL)),
        ],
        out_specs=pl.BlockSpec((1, tm, D_MODEL), lambda bi, i: (bi, i, 0)),
        out_shape=jax.ShapeDtypeStruct((b, s, D_MODEL), _F32),
        compiler_params=pltpu.CompilerParams(
            dimension_semantics=("parallel", "parallel"), vmem_limit_bytes=VMEM_LIMIT),
        name="post_mixer",
    )(x, u, u, u, attn_n, mod, pool_w_bd, pool_scale, g_pool, w_out_b, g_post,
      g_fpre, g_fpost, w_gate_b, w_up_b, w_down_b)


def _prep_layer_weights(w_in, pool_w, w_uq, w_ukv, w_out, w_gate, w_up, w_down):
    w_in_p = jnp.pad(w_in, ((0, 0), (0, IN_PAD - IN_WIDTH))).astype(_BF16)
    w_uq_p = jnp.pad(w_uq.reshape(Q_LORA_RANK, N_HEADS, QK_HEAD_DIM),
                     ((0, 0), (0, 0), (0, HEAD_PAD - QK_HEAD_DIM)))
    w_uq_p = w_uq_p.reshape(Q_LORA_RANK, N_HEADS * HEAD_PAD).astype(_BF16)
    n_grp = len(POOL_WINDOWS)
    eye = jnp.eye(n_grp, dtype=pool_w.dtype)
    pool_w_bd = (eye[:, None, :, None] * pool_w[:, :, None, :]).reshape(POOL_WIDTH, POOL_WIDTH)
    return (w_in_p, w_uq_p, w_ukv.astype(_BF16), pool_w_bd.astype(_BF16), w_out.astype(_BF16),
            w_gate.astype(_BF16), w_up.astype(_BF16), w_down.astype(_BF16))


def _encoder_layer(x, mod, tables, g_mix_pre, g_mix_post, pool_scale, g_q_a, g_kv_a,
                   g_pool_out, g_attn_out, g_ffn_pre, g_ffn_post, prepped):
    w_in_p, w_uq_p, w_ukv_b, pool_w_bd, w_out_b, w_gate_b, w_up_b, w_down_b = prepped
    row = lambda v: v.reshape(1, -1)
    s = x.shape[1]
    tabs = tuple(t[:s] for t in tables)
    u, q, kn, kr, vt = _pre_mixer(x, mod, row(g_mix_pre), w_in_p, row(g_q_a), w_uq_p,
                                  row(g_kv_a), w_ukv_b, tabs)
    attn_n = _attention(q, kn, kr, vt, row(g_attn_out))
    return _post_mixer(x, u, attn_n, mod, pool_w_bd, row(pool_scale), row(g_pool_out), w_out_b,
                       row(g_mix_post), row(g_ffn_pre), row(g_ffn_post), w_gate_b, w_up_b, w_down_b)


def kernel(x_prompt, x_sample, c_prompt, c_sample, w_ada, b_ada, g_mix_pre, g_mix_post, w_in, pool_w,
           pool_scale, g_q_a, w_uq, g_kv_a, w_ukv, g_pool_out, g_attn_out, w_out, g_ffn_pre,
           g_ffn_post, w_gate, w_up, w_down):
    depth = w_ada.shape[0]
    n_prompt = x_prompt.shape[0]
    tables = _rope_tables(max(x_prompt.shape[1], x_sample.shape[1]))
    c_all = jnp.concatenate([c_prompt, c_sample], axis=0)
    xs = [x_prompt, x_sample]
    for l in range(depth):
        mod = _adaln(c_all, w_ada[l], b_ada[l]).reshape(-1, N_MOD, D_MODEL)
        mods = [mod[:n_prompt], mod[n_prompt:]]
        prepped = _prep_layer_weights(w_in[l], pool_w[l], w_uq[l], w_ukv[l], w_out[l],
                                      w_gate[l], w_up[l], w_down[l])
        xs = [_encoder_layer(x, m, tables, g_mix_pre[l], g_mix_post[l], pool_scale[l], g_q_a[l],
                             g_kv_a[l], g_pool_out[l], g_attn_out[l], g_ffn_pre[l], g_ffn_post[l],
                             prepped)
              for x, m in zip(xs, mods)]
    return (xs[0], xs[1])
```

```python
import functools
import math

import jax
import jax.numpy as jnp
from jax import lax
from jax.experimental import pallas as pl
from jax.experimental.pallas import tpu as pltpu

D_MODEL = 1024
POOL_WIDTH = 256
POOL_WINDOWS = (2, 4, 8, 16)
POOL_GROUP_DIM = 64
N_HEADS = 6
QK_NOPE_DIM = 128
QK_ROPE_DIM = 64
QK_HEAD_DIM = QK_NOPE_DIM + QK_ROPE_DIM
V_HEAD_DIM = 128
ATTN_WIDTH = N_HEADS * V_HEAD_DIM
Q_LORA_RANK = 384
KV_LORA_RANK = 256
IN_WIDTH = POOL_WIDTH + Q_LORA_RANK + KV_LORA_RANK + QK_ROPE_DIM
D_FF = 2816
ROPE_THETA = 10000.0
EPS = 1e-6
N_MOD = 6
LOG2_E = math.log2(math.e)

LANES = 128
HEAD_PAD = 2 * LANES
IN_PAD = 1024
POOL_HALO = 8
FF_CHUNKS = ((0, 1024), (1024, 2048), (2048, 2816))
VMEM_LIMIT = 56 * 1024 * 1024

TOKEN_TILE = 512
Q_TILE = 1024
KV_TILE = 512
COL_TILE = 256
ROPE_TILE = 2048
ADA_TILE = 1024

_F32 = jnp.float32
_BF16 = jnp.bfloat16


def _rms(x, g):
    ms = jnp.mean(x * x, axis=-1, keepdims=True)
    return x * lax.rsqrt(ms + EPS) * g


def _silu(x):
    return x * (1.0 / (1.0 + jnp.exp(-x)))


def _rotate(r, ct, sa, sb):
    return r * ct + pltpu.roll(r, 96, 1) * sa + pltpu.roll(r, 32, 1) * sb


def _adaln_kernel(c_ref, w_ref, b_ref, o_ref):
    sc = _silu(c_ref[...])
    o_ref[...] = jnp.dot(sc, w_ref[...], precision=lax.Precision.HIGHEST,
                         preferred_element_type=_F32) + b_ref[...]


def _adaln(c, w_ada, b_ada):
    nb = c.shape[0]
    n_out = w_ada.shape[1]
    return pl.pallas_call(
        _adaln_kernel,
        grid=(n_out // ADA_TILE,),
        in_specs=[
            pl.BlockSpec((nb, D_MODEL), lambda j: (0, 0)),
            pl.BlockSpec((D_MODEL, ADA_TILE), lambda j: (0, j)),
            pl.BlockSpec((1, ADA_TILE), lambda j: (0, j)),
        ],
        out_specs=pl.BlockSpec((nb, ADA_TILE), lambda j: (0, j)),
        out_shape=jax.ShapeDtypeStruct((nb, n_out), _F32),
        name="adaln",
    )(c, w_ada, b_ada.reshape(1, n_out))


def _rope_table_kernel(ct_ref, sa_ref, sb_ref):
    rows = ct_ref.shape[0]
    base = pl.program_id(0) * rows
    pos = (lax.broadcasted_iota(jnp.int32, (rows, LANES), 0) + base).astype(_F32)
    lane = lax.broadcasted_iota(jnp.int32, (rows, LANES), 1)
    freq = (2 * (lane & 31)).astype(_F32)
    inv_freq = jnp.power(jnp.float32(ROPE_THETA), -freq / QK_ROPE_DIM)
    ang = pos * inv_freq
    c = jnp.cos(ang)
    s = jnp.sin(ang)
    ct_ref[...] = jnp.where(lane < 64, c, 0.0)
    sa_ref[...] = jnp.where(lane < 32, -s, 0.0)
    sb_ref[...] = jnp.where(lane < 32, 0.0, jnp.where(lane < 64, s, 0.0))


def _rope_tables(seq):
    tile = min(ROPE_TILE, seq)
    spec = pl.BlockSpec((tile, LANES), lambda i: (i, 0))
    shape = jax.ShapeDtypeStruct((seq, LANES), _F32)
    return pl.pallas_call(
        _rope_table_kernel,
        grid=(seq // tile,),
        in_specs=[],
        out_specs=[spec, spec, spec],
        out_shape=[shape, shape, shape],
        name="rope_table",
    )()


def _pre_mixer_kernel(x_ref, mod_ref, g_pre_ref, w_in_ref, g_q_ref, w_uq_ref, g_kv_ref, w_ukv_ref,
                      ct_ref, sa_ref, sb_ref,
                      u_ref, q_ref, kn_ref, kr_ref, vt_ref):
    x = x_ref[0]
    mod = mod_ref[0]
    shift1 = mod[0:1]
    scale1 = mod[1:2]
    h = _rms(x, g_pre_ref[...]) * (1.0 + scale1) + shift1
    z = jnp.dot(h.astype(_BF16), w_in_ref[...], preferred_element_type=_F32)

    u_ref[0] = z[:, :POOL_WIDTH]

    ct = ct_ref[...]
    sa = sa_ref[...]
    sb = sb_ref[...]
    qk_scale = QK_HEAD_DIM ** -0.5 * LOG2_E

    cq = z[:, POOL_WIDTH:POOL_WIDTH + Q_LORA_RANK]
    q = jnp.dot(_rms(cq, g_q_ref[...]).astype(_BF16), w_uq_ref[...], preferred_element_type=_F32)
    for hd in range(N_HEADS):
        lo = hd * HEAD_PAD
        q_nope = q[:, lo:lo + LANES] * qk_scale
        q_rope = _rotate(q[:, lo + LANES:lo + HEAD_PAD], ct, sa, sb) * qk_scale
        q_ref[0, hd, :, 0:LANES] = q_nope.astype(_BF16)
        q_ref[0, hd, :, LANES:HEAD_PAD] = q_rope.astype(_BF16)

    c0 = POOL_WIDTH + Q_LORA_RANK
    ckv = z[:, c0:c0 + KV_LORA_RANK]
    kv = jnp.dot(_rms(ckv, g_kv_ref[...]).astype(_BF16), w_ukv_ref[...], preferred_element_type=_F32)
    for hd in range(N_HEADS):
        lo = hd * (QK_NOPE_DIM + V_HEAD_DIM)
        kn_ref[0, hd] = kv[:, lo:lo + QK_NOPE_DIM].astype(_BF16)
        vt_ref[0, hd] = kv[:, lo + QK_NOPE_DIM:lo + QK_NOPE_DIM + V_HEAD_DIM].T.astype(_BF16)

    k_pe = z[:, c0 + KV_LORA_RANK:IN_PAD]
    kr_ref[0] = _rotate(k_pe, ct, sa, sb).astype(_BF16)


def _pre_mixer(x, mod, g_pre, w_in_p, g_q, w_uq_p, g_kv, w_ukv_b, tables):
    b, s, _ = x.shape
    tm = min(TOKEN_TILE, s)
    const = lambda shape: pl.BlockSpec(shape, lambda bi, i: (0,) * len(shape))
    tab = pl.BlockSpec((tm, LANES), lambda bi, i: (i, 0))
    return pl.pallas_call(
        _pre_mixer_kernel,
        grid=(b, s // tm),
        in_specs=[
            pl.BlockSpec((1, tm, D_MODEL), lambda bi, i: (bi, i, 0)),
            pl.BlockSpec((1, N_MOD, D_MODEL), lambda bi, i: (bi, 0, 0)),
            const((1, D_MODEL)),
            const((D_MODEL, IN_PAD)),
            const((1, Q_LORA_RANK)),
            const((Q_LORA_RANK, N_HEADS * HEAD_PAD)),
            const((1, KV_LORA_RANK)),
            const((KV_LORA_RANK, N_HEADS * (QK_NOPE_DIM + V_HEAD_DIM))),
            tab, tab, tab,
        ],
        out_specs=[
            pl.BlockSpec((1, tm, POOL_WIDTH), lambda bi, i: (bi, i, 0)),
            pl.BlockSpec((1, N_HEADS, tm, HEAD_PAD), lambda bi, i: (bi, 0, i, 0)),
            pl.BlockSpec((1, N_HEADS, tm, QK_NOPE_DIM), lambda bi, i: (bi, 0, i, 0)),
            pl.BlockSpec((1, tm, LANES), lambda bi, i: (bi, i, 0)),
            pl.BlockSpec((1, N_HEADS, V_HEAD_DIM, tm), lambda bi, i: (bi, 0, 0, i)),
        ],
        out_shape=[
            jax.ShapeDtypeStruct((b, s, POOL_WIDTH), _F32),
            jax.ShapeDtypeStruct((b, N_HEADS, s, HEAD_PAD), _BF16),
            jax.ShapeDtypeStruct((b, N_HEADS, s, QK_NOPE_DIM), _BF16),
            jax.ShapeDtypeStruct((b, s, LANES), _BF16),
            jax.ShapeDtypeStruct((b, N_HEADS, V_HEAD_DIM, s), _BF16),
        ],
        compiler_params=pltpu.CompilerParams(
            dimension_semantics=("parallel", "parallel"), vmem_limit_bytes=VMEM_LIMIT),
        name="pre_mixer",
    )(x, mod, g_pre, w_in_p, g_q, w_uq_p, g_kv, w_ukv_b, *tables)


def _attention_kernel(q_ref, kn_ref, kr_ref, vt_ref, g_ref, o_ref,
                      st0_ref, st1_ref, ms0_ref, ms1_ref, al0_ref, al1_ref,
                      mrun_ref, l_ref, acc_ref, *, n_kv):
    t = pl.program_id(1)
    n_steps = pl.num_programs(1)
    j_score = lax.rem(jnp.minimum(t, n_steps - 2), n_kv)
    j_fold = lax.rem(jnp.maximum(t - 1, 0), n_kv)

    @pl.when(t == 0)
    def _():
        st1_ref[...] = jnp.zeros(st1_ref.shape, _F32)
        ms1_ref[...] = jnp.zeros(ms1_ref.shape, _F32)
        al1_ref[...] = jnp.zeros(al1_ref.shape, _F32)
        l_ref[...] = jnp.zeros(l_ref.shape, _F32)
        acc_ref[...] = jnp.zeros(acc_ref.shape, _F32)

    def step(st_w, st_r, ms_w, ms_r, al_w, al_r):
        kr = kr_ref[0]
        new_tile = j_score == 0
        tq = q_ref.shape[2]
        col = min(COL_TILE, tq)
        for hd in range(N_HEADS):
            k = jnp.concatenate([kn_ref[0, hd], kr], axis=1)
            vt = vt_ref[0, hd]
            for c0 in range(0, tq, col):
                cols = slice(c0, c0 + col)
                st = lax.dot_general(k, q_ref[0, hd, cols, :], (((1,), (1,)), ((), ())),
                                     preferred_element_type=_F32)
                m_prev = jnp.where(new_tile, -1e30, mrun_ref[hd, :, cols])
                m_new = jnp.maximum(m_prev, jnp.max(st, axis=0, keepdims=True))
                mrun_ref[hd, :, cols] = m_new
                ms_w[hd, :, cols] = m_new
                al_w[hd, :, cols] = jnp.exp2(m_prev - m_new)
                st_w[hd, :, cols] = st
                p = jnp.exp2(st_r[hd, :, cols] - ms_r[hd, :, cols])
                alpha = al_r[hd, :, cols]
                l_ref[hd, :, cols] = alpha * l_ref[hd, :, cols] + jnp.sum(p, axis=0, keepdims=True)
                pv = jnp.dot(vt, p.astype(_BF16), preferred_element_type=_F32)
                acc_ref[hd, :, cols] = alpha * acc_ref[hd, :, cols] + pv

    parity = lax.rem(t, 2)

    @pl.when(parity == 0)
    def _():
        step(st0_ref, st1_ref, ms0_ref, ms1_ref, al0_ref, al1_ref)

    @pl.when(parity == 1)
    def _():
        step(st1_ref, st0_ref, ms1_ref, ms0_ref, al1_ref, al0_ref)

    @pl.when(jnp.logical_and(t > 0, j_fold == n_kv - 1))
    def _():
        outs = [acc_ref[hd] * (1.0 / l_ref[hd]) for hd in range(N_HEADS)]
        ssq = outs[0] * outs[0]
        for hd in range(1, N_HEADS):
            ssq = ssq + outs[hd] * outs[hd]
        inv = lax.rsqrt(jnp.sum(ssq, axis=0, keepdims=True) * (1.0 / ATTN_WIDTH) + EPS)
        for hd in range(N_HEADS):
            g = g_ref[:, hd * V_HEAD_DIM:(hd + 1) * V_HEAD_DIM]
            o_ref[0, :, hd * V_HEAD_DIM:(hd + 1) * V_HEAD_DIM] = ((outs[hd] * inv).T * g).astype(_BF16)


def _attention(q, kn, kr, vt, g_attn):
    b, _, s, _ = q.shape
    tq = min(Q_TILE, s)
    tk = min(KV_TILE, s)
    n_kv = s // tk
    n_blocks = (s // tq) * n_kv

    def score_idx(t):
        tc = jnp.minimum(t, n_blocks - 1)
        return tc // n_kv, tc % n_kv

    def fold_idx(t):
        tp = jnp.maximum(t - 1, 0)
        return tp // n_kv, tp % n_kv

    stat = pltpu.VMEM((N_HEADS, 1, tq), _F32)
    score = pltpu.VMEM((N_HEADS, tk, tq), _F32)
    return pl.pallas_call(
        functools.partial(_attention_kernel, n_kv=n_kv),
        grid=(b, n_blocks + 1),
        in_specs=[
            pl.BlockSpec((1, N_HEADS, tq, HEAD_PAD), lambda bi, t: (bi, 0, score_idx(t)[0], 0)),
            pl.BlockSpec((1, N_HEADS, tk, QK_NOPE_DIM), lambda bi, t: (bi, 0, score_idx(t)[1], 0)),
            pl.BlockSpec((1, tk, LANES), lambda bi, t: (bi, score_idx(t)[1], 0)),
            pl.BlockSpec((1, N_HEADS, V_HEAD_DIM, tk), lambda bi, t: (bi, 0, 0, fold_idx(t)[1])),
            pl.BlockSpec((1, ATTN_WIDTH), lambda bi, t: (0, 0)),
        ],
        out_specs=pl.BlockSpec((1, tq, ATTN_WIDTH), lambda bi, t: (bi, fold_idx(t)[0], 0)),
        out_shape=jax.ShapeDtypeStruct((b, s, ATTN_WIDTH), _BF16),
        scratch_shapes=[score, score, stat, stat, stat, stat, stat, stat,
                        pltpu.VMEM((N_HEADS, V_HEAD_DIM, tq), _F32)],
        compiler_params=pltpu.CompilerParams(
            dimension_semantics=("parallel", "arbitrary"), vmem_limit_bytes=VMEM_LIMIT),
        name="attention",
    )(q, kn, kr, vt, g_attn)


def _post_mixer_kernel(x_ref, u_ref, up_ref, un_ref, a_ref, mod_ref,
                       pw_ref, ps_ref, g_pool_ref, w_out_ref, g_post_ref,
                       g_fpre_ref, g_fpost_ref, wg_ref, wu_ref, wd_ref, y_ref):
    i = pl.program_id(1)
    n_tiles = pl.num_programs(1)
    tm = x_ref.shape[1]
    seq = tm * n_tiles
    mod = mod_ref[0]
    gate1, shift2, scale2, gate2 = mod[2:3], mod[3:4], mod[4:5], mod[5:6]

    u = u_ref[0]
    prev = jnp.where(i > 0, up_ref[0], 0.0)
    nxt = jnp.where(i < n_tiles - 1, un_ref[0], 0.0)
    ext = jnp.concatenate([prev, u, nxt], axis=0)
    rows = tm + 2 * POOL_HALO
    back = lambda a, k: pltpu.roll(a, k, 0)
    fwd = lambda a, k: pltpu.roll(a, rows - k, 0)
    s2 = ext + back(ext, 1)
    s4 = back(s2, 1) + fwd(s2, 1)
    s8 = back(s4, 2) + fwd(s4, 2)
    s16 = back(s8, 4) + fwd(s8, 4)
    lane = lax.broadcasted_iota(jnp.int32, (tm, POOL_WIDTH), 1)
    grp = lane // POOL_GROUP_DIM
    body = slice(POOL_HALO, POOL_HALO + tm)
    win = jnp.where(grp == 0, s2[body], jnp.where(grp == 1, s4[body],
                    jnp.where(grp == 2, s8[body], s16[body])))
    half = jnp.where(grp == 0, 1, jnp.where(grp == 1, 2, jnp.where(grp == 2, 4, 8)))
    t = lax.broadcasted_iota(jnp.int32, (tm, POOL_WIDTH), 0) + i * tm
    count = (jnp.minimum(t + half, seq) - jnp.maximum(t - half, 0)).astype(_F32)
    pooled = win / count - u
    mixed = jnp.dot(pooled.astype(_BF16), pw_ref[...], preferred_element_type=_F32) * ps_ref[...]
    pool_n = _rms(mixed, g_pool_ref[...])

    merged = jnp.dot(pool_n.astype(_BF16), w_out_ref[0:POOL_WIDTH, :], preferred_element_type=_F32)
    merged = merged + jnp.dot(a_ref[0], w_out_ref[POOL_WIDTH:, :], preferred_element_type=_F32)
    x1 = x_ref[0] + gate1 * _rms(merged, g_post_ref[...])

    h = (_rms(x1, g_fpre_ref[...]) * (1.0 + scale2) + shift2).astype(_BF16)
    f = None
    for lo, hi in FF_CHUNKS:
        gt = jnp.dot(h, wg_ref[:, lo:hi], preferred_element_type=_F32)
        up = jnp.dot(h, wu_ref[:, lo:hi], preferred_element_type=_F32)
        act = (_silu(gt) * up).astype(_BF16)
        part = jnp.dot(act, wd_ref[lo:hi, :], preferred_element_type=_F32)
        f = part if f is None else f + part
    y_ref[0] = x1 + gate2 * _rms(f, g_fpost_ref[...])


def _post_mixer(x, u, attn_n, mod, pool_w_bd, pool_scale, g_pool, w_out_b, g_post,
                g_fpre, g_fpost, w_gate_b, w_up_b, w_down_b):
    b, s, _ = x.shape
    tm = min(TOKEN_TILE, s)
    hb = tm // POOL_HALO
    last_hb = s // POOL_HALO - 1
    const = lambda shape: pl.BlockSpec(shape, lambda bi, i: (0,) * len(shape),
                                       pipeline_mode=pl.Buffered(1))
    return pl.pallas_call(
        _post_mixer_kernel,
        grid=(b, s // tm),
        in_specs=[
            pl.BlockSpec((1, tm, D_MODEL), lambda bi, i: (bi, i, 0)),
            pl.BlockSpec((1, tm, POOL_WIDTH), lambda bi, i: (bi, i, 0)),
            pl.BlockSpec((1, POOL_HALO, POOL_WIDTH), lambda bi, i: (bi, jnp.maximum(i * hb - 1, 0), 0)),
            pl.BlockSpec((1, POOL_HALO, POOL_WIDTH),
                         lambda bi, i: (bi, jnp.minimum((i + 1) * hb, last_hb), 0)),
            pl.BlockSpec((1, tm, ATTN_WIDTH), lambda bi, i: (bi, i, 0)),
            pl.BlockSpec((1, N_MOD, D_MODEL), lambda bi, i: (bi, 0, 0)),
            const((POOL_WIDTH, POOL_WIDTH)),
            const((1, POOL_WIDTH)),
            const((1, POOL_WIDTH)),
            const((D_MODEL, D_MODEL)),
            const((1, D_MODEL)),
            const((1, D_MODEL)),
            const((1, D_MODEL)),
            const((D_MODEL, D_FF)),
            const((D_MODEL, D_FF)),
            const((D_FF, D_MODEL)),
        ],
        out_specs=pl.BlockSpec((1, tm, D_MODEL), lambda bi, i: (bi, i, 0)),
        out_shape=jax.ShapeDtypeStruct((b, s, D_MODEL), _F32),
        compiler_params=pltpu.CompilerParams(
            dimension_semantics=("parallel", "parallel"), vmem_limit_bytes=VMEM_LIMIT),
        name="post_mixer",
    )(x, u, u, u, attn_n, mod, pool_w_bd, pool_scale, g_pool, w_out_b, g_post,
      g_fpre, g_fpost, w_gate_b, w_up_b, w_down_b)


def _prep_layer_weights(w_in, pool_w, w_uq, w_ukv, w_out, w_gate, w_up, w_down):
    w_in_p = jnp.pad(w_in, ((0, 0), (0, IN_PAD - IN_WIDTH))).astype(_BF16)
    w_uq_p = jnp.pad(w_uq.reshape(Q_LORA_RANK, N_HEADS, QK_HEAD_DIM),
                     ((0, 0), (0, 0), (0, HEAD_PAD - QK_HEAD_DIM)))
    w_uq_p = w_uq_p.reshape(Q_LORA_RANK, N_HEADS * HEAD_PAD).astype(_BF16)
    n_grp = len(POOL_WINDOWS)
    eye = jnp.eye(n_grp, dtype=pool_w.dtype)
    pool_w_bd = (eye[:, None, :, None] * pool_w[:, :, None, :]).reshape(POOL_WIDTH, POOL_WIDTH)
    return (w_in_p, w_uq_p, w_ukv.astype(_BF16), pool_w_bd.astype(_BF16), w_out.astype(_BF16),
            w_gate.astype(_BF16), w_up.astype(_BF16), w_down.astype(_BF16))


def _encoder_layer(x, mod, tables, g_mix_pre, g_mix_post, pool_scale, g_q_a, g_kv_a,
                   g_pool_out, g_attn_out, g_ffn_pre, g_ffn_post, prepped):
    w_in_p, w_uq_p, w_ukv_b, pool_w_bd, w_out_b, w_gate_b, w_up_b, w_down_b = prepped
    row = lambda v: v.reshape(1, -1)
    s = x.shape[1]
    tabs = tuple(t[:s] for t in tables)
    u, q, kn, kr, vt = _pre_mixer(x, mod, row(g_mix_pre), w_in_p, row(g_q_a), w_uq_p,
                                  row(g_kv_a), w_ukv_b, tabs)
    attn_n = _attention(q, kn, kr, vt, row(g_attn_out))
    return _post_mixer(x, u, attn_n, mod, pool_w_bd, row(pool_scale), row(g_pool_out), w_out_b,
                       row(g_mix_post), row(g_ffn_pre), row(g_ffn_post), w_gate_b, w_up_b, w_down_b)


def kernel(x_prompt, x_sample, c_prompt, c_sample, w_ada, b_ada, g_mix_pre, g_mix_post, w_in, pool_w,
           pool_scale, g_q_a, w_uq, g_kv_a, w_ukv, g_pool_out, g_attn_out, w_out, g_ffn_pre,
           g_ffn_post, w_gate, w_up, w_down):
    depth = w_ada.shape[0]
    n_prompt = x_prompt.shape[0]
    tables = _rope_tables(max(x_prompt.shape[1], x_sample.shape[1]))
    c_all = jnp.concatenate([c_prompt, c_sample], axis=0)
    xs = [x_prompt, x_sample]
    for l in range(depth):
        mod = _adaln(c_all, w_ada[l], b_ada[l]).reshape(-1, N_MOD, D_MODEL)
        mods = [mod[:n_prompt], mod[n_prompt:]]
        prepped = _prep_layer_weights(w_in[l], pool_w[l], w_uq[l], w_ukv[l], w_out[l],
                                      w_gate[l], w_up[l], w_down[l])
        xs = [_encoder_layer(x, m, tables, g_mix_pre[l], g_mix_post[l], pool_scale[l], g_q_a[l],
                             g_kv_a[l], g_pool_out[l], g_attn_out[l], g_ffn_pre[l], g_ffn_post[l],
                             prepped)
              for x, m in zip(xs, mods)]
    return (xs[0], xs[1])
```
